```python
import math
import jax
import jax.numpy as jnp
from jax import lax
import numpy as np

D_MODEL = 1024
BATCH = 4
SEQ = 4096
DEPTH = 4
DEC_BATCH = 128
DEC_SEQ = 8
PAST_LEN = 2048
PAGE_SIZE = 128

N_MIXERS = 3
LAYER_KIND = tuple(i % N_MIXERS for i in range(DEPTH))
N_LAYERS_A = sum(1 for kd in LAYER_KIND if kd == 0)
N_LAYERS_B = sum(1 for kd in LAYER_KIND if kd == 1)
N_LAYERS_C = sum(1 for kd in LAYER_KIND if kd == 2)
EPS = 1e-6

D_INNER_A = 2 * D_MODEL
CHUNK_A = 128
N_GROUPS_A = 16
GROUP_DIM_A = D_INNER_A // N_GROUPS_A

N_HEADS_B = 8
HEAD_DIM_B = D_MODEL // (2 * N_HEADS_B)
D_ATT = N_HEADS_B * 2 * HEAD_DIM_B
Q_BLOCK = 128
ALIBI_MAX_BIAS = 8.0

D_INNER_C = 2 * D_MODEL
N_HEADS_C = 16
DK_C = 128
DV_C = D_INNER_C // N_HEADS_C
D_KEY_C = N_HEADS_C * DK_C
CHUNK_C = 64

kernel_name = 'hybrid_gmlp_diffattn_hgrn2_step'


def rms_norm(x, gain):
    xf = x.astype(jnp.float32)
    y = xf * lax.rsqrt(jnp.mean(xf * xf, axis=-1, keepdims=True) + EPS)
    return (y * gain.astype(jnp.float32)).astype(x.dtype)


def gmlp_branch(h, w_in, w_vnorm, w_spatial, b_spatial):
    b, t, _ = h.shape
    proj = h @ w_in
    u = jax.nn.gelu(proj[..., :D_INNER_A])
    v = rms_norm(jax.nn.gelu(proj[..., D_INNER_A:2 * D_INNER_A]), w_vnorm)
    z = proj[..., 2 * D_INNER_A:]
    n = min(t, CHUNK_A)
    causal = jnp.tril(jnp.ones((n, n), dtype=bool))
    ws = jnp.where(causal, w_spatial[:, :n, :n], 0.0)
    vc = v.reshape(b, t // n, n, N_GROUPS_A, GROUP_DIM_A)
    mixed = jnp.einsum('gts,bcsgd->bctgd', ws, vc) + b_spatial[:, :n].T[None, None, :, :, None]
    out = u * mixed.reshape(b, t, D_INNER_A) * jax.nn.silu(z)
    return out, v


def alibi_slopes():
    return jnp.exp2(-ALIBI_MAX_BIAS * jnp.arange(1, N_HEADS_B + 1, dtype=jnp.float32) / N_HEADS_B)


def diff_qkvz(h, w_in, w_qnorm, w_knorm):
    b, t, _ = h.shape
    q, k, v, z = jnp.split(h @ w_in, 4, axis=-1)
    q = rms_norm(q.reshape(b, t, N_HEADS_B, 2, HEAD_DIM_B), w_qnorm)
    k = rms_norm(k.reshape(b, t, N_HEADS_B, 2, HEAD_DIM_B), w_knorm)
    v = v.reshape(b, t, N_HEADS_B, 2 * HEAD_DIM_B)
    return q, k, v, z


def diff_lambda(w_lambda, layer_idx):
    lam_init = 0.8 - 0.6 * math.exp(-0.3 * layer_idx)
    wl = w_lambda.astype(jnp.float32)
    lam = jnp.exp(jnp.sum(wl[0] * wl[1])) - jnp.exp(jnp.sum(wl[2] * wl[3])) + lam_init
    return lam, lam_init


def diff_attend(q, q_pos, segments, lam, slopes):
    scale = HEAD_DIM_B ** -0.5
    scores = []
    for k, _, k_pos in segments:
        s = jnp.einsum('bqhmd,bkhmd->bhmqk', q, k).astype(jnp.float32) * scale
        dist = (q_pos[:, None] - k_pos[None, :]).astype(jnp.float32)
        bias = jnp.where(dist >= 0, -slopes[:, None, None] * dist, -jnp.inf)
        scores.append(s + bias[None, :, None])
    p = jax.nn.softmax(jnp.concatenate(scores, axis=-1), axis=-1)
    p = p[:, :, 0] - lam * p[:, :, 1]
    outs = []
    start = 0
    for _, v, k_pos in segments:
        n = k_pos.shape[0]
        outs.append(jnp.einsum('bhqk,bkhe->bqhe', p[..., start:start + n].astype(v.dtype), v))
        start += n
    return sum(outs)


def diff_finish(o, z, w_subln, lam_init):
    b, t = o.shape[:2]
    o = rms_norm(o, w_subln) * (1.0 - lam_init)
    return o.reshape(b, t, D_ATT) * jax.nn.silu(z)


def hgrn2_lower_bounds(w_lb):
    sm = jax.nn.softmax(w_lb.astype(jnp.float32), axis=0)
    return jnp.cumsum(sm, axis=0) - sm[0]


def hgrn2_inputs(h, w_in, lb):
    b, t, _ = h.shape
    proj = h @ w_in
    q = proj[..., :D_KEY_C].astype(jnp.float32)
    f = proj[..., D_KEY_C:2 * D_KEY_C].astype(jnp.float32)
    i = proj[..., 2 * D_KEY_C:2 * D_KEY_C + D_INNER_C].astype(jnp.float32)
    z = proj[..., 2 * D_KEY_C + D_INNER_C:]
    q = (jax.nn.silu(q) * DK_C ** -0.5).reshape(b, t, N_HEADS_C, DK_C)
    g = jnp.logaddexp(jnp.log(lb), jnp.log1p(-lb) + jax.nn.log_sigmoid(f)).reshape(b, t, N_HEADS_C, DK_C)
    k = ((1.0 - lb) * jax.nn.sigmoid(-f)).reshape(b, t, N_HEADS_C, DK_C)
    v = i.reshape(b, t, N_HEADS_C, DV_C)
    return q, k, g, v, z


def hgrn2_chunk(s0, q, k, g, v):
    n = q.shape[1]
    cum = jnp.cumsum(g, axis=1)
    causal = jnp.tril(jnp.ones((n, n), dtype=bool))[None, :, :, None, None]
    decay = jnp.exp(jnp.where(causal, cum[:, :, None] - cum[:, None, :], -jnp.inf))
    scores = jnp.einsum('bthd,bshd,btshd->bhts', q, k, decay)
    o = jnp.einsum('bhts,bshe->bthe', scores, v) + jnp.einsum('bthd,bhde->bthe', q * jnp.exp(cum), s0)
    last = cum[:, -1]
    s_new = jnp.exp(last)[..., None] * s0 + jnp.einsum('bshd,bshe->bhde', k * jnp.exp(last[:, None] - cum), v)
    return s_new, o


def hgrn2_prompt(q, k, g, v):
    b, t = q.shape[:2]
    n_chunks = t // CHUNK_C

    def to_chunks(a):
        return jnp.moveaxis(a.reshape(b, n_chunks, CHUNK_C, *a.shape[2:]), 1, 0)

    def step(s, xs):
        return hgrn2_chunk(s, *xs)

    s0 = jnp.zeros((b, N_HEADS_C, DK_C, DV_C), jnp.float32)
    s_final, o = lax.scan(step, s0, (to_chunks(q), to_chunks(k), to_chunks(g), to_chunks(v)))
    return s_final, jnp.moveaxis(o, 0, 1).reshape(b, t, N_HEADS_C, DV_C)


def hgrn2_finish(o, z, w_gnorm):
    b, t = o.shape[:2]
    o = rms_norm(o, w_gnorm).astype(z.dtype)
    return o.reshape(b, t, D_INNER_C) * jax.nn.silu(z)


def _normal(key, shape, scale=1.0):
    return scale * jax.random.normal(key, shape, jnp.float32)


def setup_inputs(seed: int = 0) -> dict:
    key = jax.random.key(seed)
    ks = jax.random.split(key, 24)
    n_pages = PAST_LEN // PAGE_SIZE
    n_used = DEC_BATCH * n_pages
    n_phys = n_used + n_used // 4
    page_table = jax.random.permutation(ks[0], n_phys)[:n_used].reshape(DEC_BATCH, n_pages).astype(jnp.int32)
    return {
        'x_prompt': _normal(ks[1], (BATCH, SEQ, D_MODEL)),
        'x_sample': _normal(ks[2], (DEC_BATCH, DEC_SEQ, D_MODEL)),
        'cache_k': _normal(ks[3], (N_LAYERS_B, n_phys, PAGE_SIZE, N_HEADS_B, 2 * HEAD_DIM_B)),
        'cache_v': _normal(ks[4], (N_LAYERS_B, n_phys, PAGE_SIZE, N_HEADS_B, 2 * HEAD_DIM_B)),
        'state_hgrn': _normal(ks[5], (N_LAYERS_C, DEC_BATCH, N_HEADS_C, DK_C, DV_C), 0.5),
        'page_table': page_table,
        'w_norm': 1.0 + _normal(ks[6], (DEPTH, D_MODEL), 0.02),
        'w_a_in': _normal(ks[7], (N_LAYERS_A, D_MODEL, 3 * D_INNER_A), D_MODEL ** -0.5),
        'w_a_vnorm': 1.0 + _normal(ks[8], (N_LAYERS_A, D_INNER_A), 0.02),
        'w_a_spatial': _normal(ks[9], (N_LAYERS_A, N_GROUPS_A, CHUNK_A, CHUNK_A), CHUNK_A ** -0.5),
        'b_a_spatial': 1.0 + _normal(ks[10], (N_LAYERS_A, N_GROUPS_A, CHUNK_A), 0.1),
        'w_a_out': _normal(ks[11], (N_LAYERS_A, D_INNER_A, D_MODEL), D_INNER_A ** -0.5),
        'w_b_in': _normal(ks[12], (N_LAYERS_B, D_MODEL, 4 * D_ATT), D_MODEL ** -0.5),
        'w_b_qnorm': 1.0 + _normal(ks[13], (N_LAYERS_B, HEAD_DIM_B), 0.02),
        'w_b_knorm': 1.0 + _normal(ks[14], (N_LAYERS_B, HEAD_DIM_B), 0.02),
        'w_b_lambda': _normal(ks[15], (N_LAYERS_B, 4, HEAD_DIM_B), 0.1),
        'w_b_subln': 1.0 + _normal(ks[16], (N_LAYERS_B, 2 * HEAD_DIM_B), 0.02),
        'w_b_out': _normal(ks[17], (N_LAYERS_B, D_ATT, D_MODEL), D_ATT ** -0.5),
        'w_c_in': _normal(ks[18], (N_LAYERS_C, D_MODEL, 2 * D_KEY_C + 2 * D_INNER_C), D_MODEL ** -0.5),
        'w_c_lb': _normal(ks[19], (DEPTH, D_KEY_C), 0.5),
        'w_c_gnorm': 1.0 + _normal(ks[20], (N_LAYERS_C, DV_C), 0.02),
        'w_c_out': _normal(ks[21], (N_LAYERS_C, D_INNER_C, D_MODEL), D_INNER_C ** -0.5),
    }


def reference(x_prompt, x_sample, cache_k, cache_v, state_hgrn, page_table,
              w_norm, w_a_in, w_a_vnorm, w_a_spatial, b_a_spatial, w_a_out,
              w_b_in, w_b_qnorm, w_b_knorm, w_b_lambda, w_b_subln, w_b_out,
              w_c_in, w_c_lb, w_c_gnorm, w_c_out):
    dec_batch, dec_seq = x_sample.shape[:2]
    seq = x_prompt.shape[1]
    past_len = page_table.shape[1] * PAGE_SIZE
    pos_prompt = jnp.arange(seq)
    pos_past = jnp.arange(past_len)
    pos_new = past_len + jnp.arange(dec_seq)
    slopes = alibi_slopes()
    lower_bounds = hgrn2_lower_bounds(w_c_lb)

    yp, ys = x_prompt, x_sample
    a_v_s, k_p, v_p, k_s, v_s, c_p, c_s = [], [], [], [], [], [], []
    for i in range(DEPTH):
        j = i // N_MIXERS
        hp = rms_norm(yp, w_norm[i])
        hs = rms_norm(ys, w_norm[i])
        if LAYER_KIND[i] == 0:
            bp, _ = gmlp_branch(hp, w_a_in[j], w_a_vnorm[j], w_a_spatial[j], b_a_spatial[j])
            bs, v_rows = gmlp_branch(hs, w_a_in[j], w_a_vnorm[j], w_a_spatial[j], b_a_spatial[j])
            a_v_s.append(v_rows)
            w_out = w_a_out[j]
        elif LAYER_KIND[i] == 1:
            lam, lam_init = diff_lambda(w_b_lambda[j], i)
            qp, kp, vp, zp = diff_qkvz(hp, w_b_in[j], w_b_qnorm[j], w_b_knorm[j])
            n_blk = seq // Q_BLOCK
            q_blocks = jnp.moveaxis(qp.reshape(qp.shape[0], n_blk, Q_BLOCK, N_HEADS_B, 2, HEAD_DIM_B), 1, 0)

            def attend_block(args, kp=kp, vp=vp, lam=lam):
                q_blk, start = args
                return diff_attend(q_blk, start + jnp.arange(Q_BLOCK), ((kp, vp, pos_prompt),), lam, slopes)

            op = lax.map(attend_block, (q_blocks, jnp.arange(n_blk) * Q_BLOCK))
            op = jnp.moveaxis(op, 0, 1).reshape(vp.shape)
            bp = diff_finish(op, zp, w_b_subln[j], lam_init)

            qs, ks_, vs_, zs = diff_qkvz(hs, w_b_in[j], w_b_qnorm[j], w_b_knorm[j])
            k_past = cache_k[j, page_table].reshape(dec_batch, past_len, N_HEADS_B, 2, HEAD_DIM_B)
            v_past = cache_v[j, page_table].reshape(dec_batch, past_len, N_HEADS_B, 2 * HEAD_DIM_B)
            os_ = diff_attend(qs, pos_new, ((k_past, v_past, pos_past), (ks_, vs_, pos_new)), lam, slopes)
            bs = diff_finish(os_, zs, w_b_subln[j], lam_init)
            k_p.append(kp.reshape(vp.shape))
            v_p.append(vp)
            k_s.append(ks_.reshape(vs_.shape))
            v_s.append(vs_)
            w_out = w_b_out[j]
        else:
            qp, kp, gp, vp, zp = hgrn2_inputs(hp, w_c_in[j], lower_bounds[i])
            sp, op = hgrn2_prompt(qp, kp, gp, vp)
            bp = hgrn2_finish(op, zp, w_c_gnorm[j])
            qs, ks_, gs, vs_, zs = hgrn2_inputs(hs, w_c_in[j], lower_bounds[i])
            ss, os_ = hgrn2_chunk(state_hgrn[j].astype(jnp.float32), qs, ks_, gs, vs_)
            bs = hgrn2_finish(os_, zs, w_c_gnorm[j])
            c_p.append(sp.astype(x_prompt.dtype))
            c_s.append(ss.astype(state_hgrn.dtype))
            w_out = w_c_out[j]
        yp = yp + bp @ w_out
        ys = ys + bs @ w_out

    new_a_v_sample = jnp.stack(a_v_s)
    new_k_prompt = jnp.stack(k_p)
    new_v_prompt = jnp.stack(v_p)
    new_k_sample = jnp.stack(k_s)
    new_v_sample = jnp.stack(v_s)
    new_hgrn_prompt = jnp.stack(c_p)
    new_hgrn_sample = jnp.stack(c_s)
    return (yp, ys, new_a_v_sample, new_k_prompt, new_v_prompt, new_k_sample, new_v_sample, new_hgrn_prompt, new_hgrn_sample)
```

```python
import functools
import math

import jax
import jax.numpy as jnp
from jax import lax
from jax.experimental import pallas as pl
from jax.experimental.pallas import tpu as pltpu

F32 = jnp.float32
BF16 = jnp.bfloat16
EPS = 1e-6
ALIBI_MAX_BIAS = 8.0
N_MIXERS = 3
LANES = 128
SUBLANES = 8
VMEM_LIMIT = 56 * 1024 * 1024
NEG = -1e30


def _cparams(n_grid):
    return pltpu.CompilerParams(
        dimension_semantics=("arbitrary",) * n_grid, vmem_limit_bytes=VMEM_LIMIT)


def _dot(a, b):
    return jnp.dot(a, b, preferred_element_type=F32)


def _dot_nt(a, b):
    return lax.dot_general(a, b, (((1,), (1,)), ((), ())), preferred_element_type=F32)


def _rms(x, gain):
    return x * lax.rsqrt(jnp.mean(x * x, axis=-1, keepdims=True) + EPS) * gain


def _sigmoid(x):
    return 1.0 / (1.0 + jnp.exp(-x))


def _silu(x):
    return x * _sigmoid(x)


def _gelu(x):
    c = math.sqrt(2.0 / math.pi)
    return 0.5 * x * (1.0 + jnp.tanh(c * (x + 0.044715 * (x * x * x))))


def _idiv(x, n):
    if n & (n - 1) == 0:
        return lax.shift_right_logical(x, jnp.int32(n.bit_length() - 1))
    return x // n


def _imod(x, n):
    if n & (n - 1) == 0:
        return x & (n - 1)
    return x % n


def _const_spec(shape):
    return pl.BlockSpec(shape, lambda *_: (0,) * len(shape))


def _a_layer_kernel(x_ref, gain_ref, win_ref, vg_ref, wsp_ref, bsp_ref, wout_ref,
                    *rest, seg, chunk, d_inner, write_v, col):
    if write_v:
        y_ref, vout_ref, u_s, v_s, z_s, g_s = rest
    else:
        y_ref, u_s, v_s, z_s, g_s = rest
    tm = x_ref.shape[0]
    x = x_ref[...]
    h = _rms(x, gain_ref[...]).astype(BF16)
    ssq = jnp.zeros((tm, 1), F32)
    for c in range(d_inner // col):
        lo, hi = c * col, (c + 1) * col
        u_s[:, lo:hi] = _gelu(_dot(h, win_ref[:, lo:hi]))
        v = _gelu(_dot(h, win_ref[:, d_inner + lo:d_inner + hi]))
        v_s[:, lo:hi] = v
        ssq = ssq + jnp.sum(v * v, axis=-1, keepdims=True)
        z_s[:, lo:hi] = _silu(_dot(h, win_ref[:, 2 * d_inner + lo:2 * d_inner + hi]))
    vn = v_s[...] * lax.rsqrt(ssq * (1.0 / d_inner) + EPS) * vg_ref[...]
    if write_v:
        vout_ref[...] = vn
    v_s[...] = vn
    n_groups = wsp_ref.shape[0]
    gd = d_inner // n_groups
    row = lax.broadcasted_iota(jnp.int32, (chunk, chunk), 0)
    cl = lax.broadcasted_iota(jnp.int32, (chunk, chunk), 1)
    mask = (cl <= row) & (cl >= _idiv(row, seg) * seg)
    for g in range(n_groups):
        w = jnp.where(mask, wsp_ref[g], 0.0).astype(BF16)
        b = bsp_ref[g]
        for r in range(tm // chunk):
            rs = slice(r * chunk, (r + 1) * chunk)
            cs = slice(g * gd, (g + 1) * gd)
            mixed = _dot(w, v_s[rs, cs].astype(BF16)) + b
            g_s[rs, cs] = (u_s[rs, cs] * mixed * z_s[rs, cs]).astype(BF16)
    y_ref[...] = x + _dot(g_s[...], wout_ref[...])


def _a_layer(x2d, gain, w_in, w_vnorm, w_spatial, b_spatial, w_out, *, seg, write_v):
    m, d = x2d.shape
    d_inner = w_out.shape[0]
    n_groups, chunk = w_spatial.shape[0], w_spatial.shape[1]
    gd = d_inner // n_groups
    tm = min(256, m)
    reps = chunk // seg
    wsp = jnp.tile(w_spatial[:, :seg, :seg], (1, reps, reps))
    bsp = jnp.broadcast_to(jnp.tile(b_spatial[:, :seg], (1, reps))[:, :, None], (n_groups, chunk, gd))
    out_shape = [jax.ShapeDtypeStruct((m, d), F32)]
    out_specs = [pl.BlockSpec((tm, d), lambda i: (i, 0))]
    if write_v:
        out_shape.append(jax.ShapeDtypeStruct((m, d_inner), F32))
        out_specs.append(pl.BlockSpec((tm, d_inner), lambda i: (i, 0)))
    kern = functools.partial(_a_layer_kernel, seg=seg, chunk=chunk, d_inner=d_inner,
                             write_v=write_v, col=min(512, d_inner))
    res = pl.pallas_call(
        kern,
        grid=(m // tm,),
        in_specs=[
            pl.BlockSpec((tm, d), lambda i: (i, 0)),
            _const_spec((1, d)),
            _const_spec((d, 3 * d_inner)),
            _const_spec((1, d_inner)),
            _const_spec((n_groups, chunk, chunk)),
            _const_spec((n_groups, chunk, gd)),
            _const_spec((d_inner, d)),
        ],
        out_specs=out_specs,
        out_shape=out_shape,
        scratch_shapes=[pltpu.VMEM((tm, d_inner), F32), pltpu.VMEM((tm, d_inner), F32),
                        pltpu.VMEM((tm, d_inner), F32), pltpu.VMEM((tm, d_inner), BF16)],
        compiler_params=_cparams(1),
        name="a_layer_sample" if write_v else "a_layer_prompt",
    )(x2d, gain.reshape(1, d), w_in.astype(BF16), w_vnorm.reshape(1, d_inner), wsp, bsp,
      w_out.astype(BF16))
    return res


def _proj_kernel(x_ref, gain_ref, w_ref, *rest, epilogue, n_extra, col, head_major):
    extra = rest[:n_extra]
    outs = rest[n_extra:]
    n = w_ref.shape[1]
    h = _rms(x_ref[...], gain_ref[...]).astype(BF16)
    for c in range(n // col):
        lo, hi = c * col, (c + 1) * col
        p = _dot(h, w_ref[:, lo:hi])
        res = epilogue(p, lo, hi, *extra)
        for o_ref, r in zip(outs, res):
            if head_major:
                for j in range(col // LANES):
                    o_ref[lo // LANES + j] = r[:, j * LANES:(j + 1) * LANES].astype(o_ref.dtype)
            else:
                o_ref[:, lo:hi] = r.astype(o_ref.dtype)


def _proj(x2d, gain, w_slab, epilogue, extras, n_out, *, head_major=False, name="proj"):
    m, d = x2d.shape
    n = w_slab.shape[1]
    tm = min(512, m)
    col = min(512, n)
    if head_major:
        out_shape = [jax.ShapeDtypeStruct((n // LANES, m, LANES), F32)] * n_out
        out_specs = [pl.BlockSpec((n // LANES, tm, LANES), lambda i: (0, i, 0))] * n_out
    else:
        out_shape = [jax.ShapeDtypeStruct((m, n), F32)] * n_out
        out_specs = [pl.BlockSpec((tm, n), lambda i: (i, 0))] * n_out
    kern = functools.partial(_proj_kernel, epilogue=epilogue, n_extra=len(extras), col=col,
                             head_major=head_major)
    return pl.pallas_call(
        kern,
        grid=(m // tm,),
        in_specs=[pl.BlockSpec((tm, d), lambda i: (i, 0)), _const_spec((1, d)),
                  _const_spec((d, n))] + [_const_spec(e.shape) for e in extras],
        out_specs=out_specs,
        out_shape=out_shape,
        compiler_params=_cparams(1),
        name=name,
    )(x2d, gain.reshape(1, d), w_slab.astype(BF16), *extras)


def _ep_identity(p, lo, hi):
    return (p,)


def _ep_qknorm(p, lo, hi, gain_ref, gsum_ref, *, scale, head_dim):
    w = gsum_ref.shape[0]
    parts = []
    for j in range((hi - lo) // w):
        pj = p[:, j * w:(j + 1) * w]
        ssq = _dot((pj * pj).astype(BF16), gsum_ref[...])
        parts.append(pj * lax.rsqrt(ssq * (1.0 / head_dim) + EPS))
    pn = jnp.concatenate(parts, axis=-1) if len(parts) > 1 else parts[0]
    return (pn * (gain_ref[:, lo:hi] * scale),)


def _ep_c_q(p, lo, hi, *, scale):
    return (_silu(p) * scale,)


def _lower_bound(wlb_ref, layer, lo, hi):
    w = wlb_ref[:, lo:hi]
    e = jnp.exp(w - jnp.max(w, axis=0, keepdims=True))
    den = jnp.sum(e, axis=0, keepdims=True)
    num = jnp.zeros_like(den)
    for r in range(1, layer + 1):
        num = num + e[r:r + 1, :]
    return num / den


def _ep_c_f(p, lo, hi, wlb_ref, *, layer):
    lb = _lower_bound(wlb_ref, layer, lo, hi)
    sg = _sigmoid(p)
    k = (1.0 - lb) * (1.0 - sg)
    g = jnp.log(lb + (1.0 - lb) * sg)
    return (k, g)


def _finish_kernel(o_ref, z_ref, x_ref, gain_ref, w_ref, y_ref, *, head_major, n_heads, cst):
    parts = []
    for hh in range(n_heads):
        cs = slice(hh * LANES, (hh + 1) * LANES)
        oh = o_ref[hh] if head_major else o_ref[:, cs]
        on = _rms(oh, gain_ref[...]) * cst
        parts.append((on * _silu(z_ref[:, cs])).astype(BF16))
    gated = jnp.concatenate(parts, axis=-1)
    y_ref[...] = x_ref[...] + _dot(gated, w_ref[...])


def _finish(o, z, x2d, gain, w_out, *, head_major, cst, name):
    m, d = x2d.shape
    w = w_out.shape[0]
    n_heads = w // LANES
    tm = min(256, m)
    if head_major:
        o_spec = pl.BlockSpec((n_heads, tm, LANES), lambda i: (0, i, 0))
    else:
        o_spec = pl.BlockSpec((tm, w), lambda i: (i, 0))
    kern = functools.partial(_finish_kernel, head_major=head_major, n_heads=n_heads, cst=cst)
    return pl.pallas_call(
        kern,
        grid=(m // tm,),
        in_specs=[o_spec, pl.BlockSpec((tm, w), lambda i: (i, 0)),
                  pl.BlockSpec((tm, d), lambda i: (i, 0)), _const_spec((1, LANES)),
                  _const_spec((w, d))],
        out_specs=pl.BlockSpec((tm, d), lambda i: (i, 0)),
        out_shape=jax.ShapeDtypeStruct((m, d), F32),
        compiler_params=_cparams(1),
        name=name,
    )(o, z, x2d, gain.reshape(1, LANES), w_out.astype(BF16))


def _lambda(wl_ref):
    wl = wl_ref[...]
    a = jnp.sum(wl[0:1] * wl[1:2], axis=-1, keepdims=True)
    b = jnp.sum(wl[2:3] * wl[3:4], axis=-1, keepdims=True)
    return jnp.exp(a) - jnp.exp(b)


def _attn_prompt_kernel(q_ref, k_ref, v_ref, wl_ref, o_ref, *, tq, tk, head_dim, n_heads, lam_init):
    hh = pl.program_id(1)
    qi = pl.program_id(2)
    slope = jnp.exp2(jnp.full((1, 1), hh + 1, jnp.int32).astype(F32) * (-ALIBI_MAX_BIAS / n_heads))
    lam = _lambda(wl_ref) + lam_init
    q = q_ref[...].astype(BF16)
    lane = lax.broadcasted_iota(jnp.int32, q.shape, 1)
    q0 = jnp.where(lane < head_dim, q, jnp.zeros_like(q))
    q1 = jnp.where(lane >= head_dim, q, jnp.zeros_like(q))
    rel = (lax.broadcasted_iota(jnp.int32, (tq, tk), 0)
           - lax.broadcasted_iota(jnp.int32, (tq, tk), 1))

    def update(s, m, l, acc, vb):
        m_new = jnp.maximum(m, jnp.max(s, axis=-1, keepdims=True))
        alpha = jnp.exp(m - m_new)
        p = jnp.exp(s - m_new)
        l = alpha * l + jnp.sum(p, axis=-1, keepdims=True)
        acc = alpha * acc + _dot(p.astype(BF16), vb)
        return m_new, l, acc

    def body(kj, carry):
        m0, l0, a0, m1, l1, a1 = carry
        ks = pl.multiple_of(kj * tk, tk)
        kb = k_ref[pl.ds(ks, tk), :].astype(BF16)
        vb = v_ref[pl.ds(ks, tk), :].astype(BF16)
        dist = rel + (qi * tq - kj * tk)
        bias = jnp.where(dist >= 0, -slope * dist.astype(F32), NEG)
        m0, l0, a0 = update(_dot_nt(q0, kb) + bias, m0, l0, a0, vb)
        m1, l1, a1 = update(_dot_nt(q1, kb) + bias, m1, l1, a1, vb)
        return m0, l0, a0, m1, l1, a1

    mi = jnp.full((tq, 1), NEG, F32)
    li = jnp.zeros((tq, 1), F32)
    ai = jnp.zeros((tq, 2 * head_dim), F32)
    n_kv = (qi * tq + tq + tk - 1) // tk
    m0, l0, a0, m1, l1, a1 = lax.fori_loop(0, n_kv, body, (mi, li, ai, mi, li, ai))
    o_ref[...] = a0 / l0 - lam * (a1 / l1)


def _attn_prompt(q2d, k2d, v2d, w_lambda, *, batch, seq, head_dim, lam_init):
    d_att = q2d.shape[1]
    hw = 2 * head_dim
    n_heads = d_att // hw
    tq = tk = min(256, seq)
    nq = seq // tq
    kern = functools.partial(_attn_prompt_kernel, tq=tq, tk=tk, head_dim=head_dim,
                             n_heads=n_heads, lam_init=lam_init)
    return pl.pallas_call(
        kern,
        grid=(batch, n_heads, nq),
        in_specs=[pl.BlockSpec((tq, hw), lambda b, h, i: (b * nq + i, h)),
                  pl.BlockSpec((seq, hw), lambda b, h, i: (b, h)),
                  pl.BlockSpec((seq, hw), lambda b, h, i: (b, h)),
                  _const_spec(w_lambda.shape)],
        out_specs=pl.BlockSpec((tq, hw), lambda b, h, i: (b * nq + i, h)),
        out_shape=jax.ShapeDtypeStruct((batch * seq, d_att), F32),
        compiler_params=_cparams(3),
        name="attn_prompt",
    )(q2d, k2d, v2d, w_lambda)


def _attn_sample_kernel(pt_ref, q_ref, kn_ref, vn_ref, wl_ref, *rest, n_pages, page, head_dim,
                        n_heads, lam_init, t_new):
    k_pages = rest[:n_pages]
    v_pages = rest[n_pages:2 * n_pages]
    o_ref = rest[2 * n_pages]
    s_s = rest[2 * n_pages + 1]
    hw = 2 * head_dim
    half = n_heads * t_new
    rows = 2 * half
    cols = page * n_heads
    past = n_pages * page
    lam = _lambda(wl_ref) + lam_init

    def by_head(x):
        return jnp.concatenate([x[:, hh * hw:(hh + 1) * hw] for hh in range(n_heads)], axis=0)

    qh = by_head(q_ref[...])
    r = lax.broadcasted_iota(jnp.int32, (rows, hw), 0)
    ln = lax.broadcasted_iota(jnp.int32, (rows, hw), 1)
    q2 = jnp.where((ln >= head_dim) == (r >= half), jnp.concatenate([qh, qh], axis=0), 0.0)
    q2 = q2.astype(BF16)

    def row_info(width):
        rr = lax.broadcasted_iota(jnp.int32, (rows, width), 0)
        r_head = _imod(_idiv(rr, t_new), n_heads)
        slope = jnp.exp2((r_head + 1).astype(F32) * (-ALIBI_MAX_BIAS / n_heads))
        return _imod(rr, t_new), r_head, slope, lax.broadcasted_iota(jnp.int32, (rows, width), 1)

    r_tok, r_head, slope, cc = row_info(cols)
    base = jnp.where(_imod(cc, n_heads) == r_head,
                     -slope * (past + r_tok - _idiv(cc, n_heads)).astype(F32), NEG)
    slope_col = row_info(1)[2]
    ncn = -(-half // LANES) * LANES
    r_tok, r_head, slope, cn = row_info(ncn)
    c_tok = _imod(cn, t_new)
    base_new = jnp.where((_idiv(cn, t_new) == r_head) & (c_tok <= r_tok),
                         -slope * (r_tok - c_tok).astype(F32), NEG)

    def page_shift(p):
        return slope_col * float(p * page)

    mx = jnp.full((rows, 1), NEG, F32)
    for p in range(n_pages):
        kp = k_pages[p][...].reshape(cols, hw).astype(BF16)
        s = _dot_nt(q2, kp) + base
        s_s[p] = s
        mx = jnp.maximum(mx, jnp.max(s, axis=-1, keepdims=True) + page_shift(p))
    zpad = jnp.zeros((ncn - half, hw), F32)
    kn = jnp.concatenate([by_head(kn_ref[...]), zpad], axis=0).astype(BF16)
    vn = jnp.concatenate([by_head(vn_ref[...]), zpad], axis=0).astype(BF16)
    s = _dot_nt(q2, kn) + base_new
    mx = jnp.maximum(mx, jnp.max(s, axis=-1, keepdims=True))

    pr = jnp.exp(s - mx)
    den = jnp.sum(pr, axis=-1, keepdims=True)
    acc = _dot(pr.astype(BF16), vn)
    for p in range(n_pages):
        pr = jnp.exp(s_s[p] - (mx - page_shift(p)))
        den = den + jnp.sum(pr, axis=-1, keepdims=True)
        acc = acc + _dot(pr.astype(BF16), v_pages[p][...].reshape(cols, hw).astype(BF16))
    rmap = lax.broadcasted_iota(jnp.int32, (rows, 1), 0)
    acc = acc * (jnp.where(rmap < half, 1.0, -lam) / den)
    o_ref[...] = jnp.concatenate(
        [acc[hh * t_new:(hh + 1) * t_new] + acc[half + hh * t_new:half + (hh + 1) * t_new]
         for hh in range(n_heads)], axis=-1)


def _attn_sample(q2d, k2d, v2d, cache_k, cache_v, page_table, layer, w_lambda, *,
                 t_new, head_dim, lam_init):
    m, d_att = q2d.shape
    dec_batch, n_pages = page_table.shape
    page, n_heads, hw = cache_k.shape[2:]
    rows = 2 * n_heads * t_new

    def page_spec(p):
        return pl.BlockSpec((None, None, page, n_heads, hw),
                            lambda b, pt: (layer, pt[b, p], 0, 0, 0))

    tok_spec = pl.BlockSpec((t_new, d_att), lambda b, pt: (b, 0))
    kern = functools.partial(_attn_sample_kernel, n_pages=n_pages, page=page, head_dim=head_dim,
                             n_heads=n_heads, lam_init=lam_init, t_new=t_new)
    grid_spec = pltpu.PrefetchScalarGridSpec(
        num_scalar_prefetch=1,
        grid=(dec_batch,),
        in_specs=[tok_spec, tok_spec, tok_spec,
                  pl.BlockSpec(w_lambda.shape, lambda b, pt: (0, 0))]
        + [page_spec(p) for p in range(n_pages)] * 2,
        out_specs=tok_spec,
        scratch_shapes=[pltpu.VMEM((n_pages, rows, page * n_heads), F32)],
    )
    return pl.pallas_call(
        kern,
        grid_spec=grid_spec,
        out_shape=jax.ShapeDtypeStruct((m, d_att), F32),
        compiler_params=_cparams(1),
        name="attn_sample",
    )(page_table, q2d, k2d, v2d, w_lambda, *([cache_k] * n_pages), *([cache_v] * n_pages))


def _cumsum_rows(g):
    n = g.shape[0]
    row = lax.broadcasted_iota(jnp.int32, g.shape, 0)
    y = g
    sh = 1
    while sh < n:
        if sh < SUBLANES:
            shifted = pltpu.roll(y, sh, 0)
        else:
            shifted = jnp.concatenate([y[n - sh:], y[:n - sh]], axis=0)
        y = y + jnp.where(row >= sh, shifted, 0.0)
        sh *= 2
    return y


def _pair_reference(cum, b):
    n, w = cum.shape
    c3 = cum.reshape(n // SUBLANES, SUBLANES, w)

    def bc(j):
        return jnp.broadcast_to(c3[:, j:j + 1, :], c3.shape)

    if 2 * b <= SUBLANES:
        local = lax.broadcasted_iota(jnp.int32, c3.shape, 1)
        out = bc(b - 1)
        for j in range(1, SUBLANES // (2 * b)):
            out = jnp.where(local >= j * 2 * b, bc(j * 2 * b + b - 1), out)
        return out.reshape(n, w)
    last = bc(SUBLANES - 1)
    per = 2 * b // SUBLANES
    pieces = []
    for v in range(n // SUBLANES):
        src = (v // per) * per + b // SUBLANES - 1
        pieces.append(last[src])
    return jnp.concatenate(pieces, axis=0)


def _intra_chunk(q, k, cum):
    n = q.shape[0]
    row = lax.broadcasted_iota(jnp.int32, (n, n), 0)
    cl = lax.broadcasted_iota(jnp.int32, (n, n), 1)
    a = jnp.where(row == cl, _dot_nt(q.astype(BF16), k.astype(BF16)), 0.0)
    b = 1
    while b < n:
        e = jnp.exp(-jnp.abs(cum - _pair_reference(cum, b)))
        s = _dot_nt((q * e).astype(BF16), (k * e).astype(BF16))
        sel = ((_idiv(row, 2 * b) == _idiv(cl, 2 * b)) & (_imod(_idiv(row, b), 2) == 1)
               & (_imod(_idiv(cl, b), 2) == 0))
        a = jnp.where(sel, s, a)
        b *= 2
    return a


def _rec_prompt_kernel(q_ref, k_ref, g_ref, v_ref, o_ref, s_out_ref, st_s, *, n_heads):
    c = pl.program_id(1)
    nc = pl.num_programs(1)

    @pl.when(c == 0)
    def _():
        st_s[...] = jnp.zeros_like(st_s)

    def head(hh, carry):
        q = q_ref[hh]
        k = k_ref[hh]
        v = v_ref[hh]
        cum = _cumsum_rows(g_ref[hh])
        n = q.shape[0]
        st = st_s[hh]
        a = _intra_chunk(q, k, cum)
        o = _dot(a.astype(BF16), v.astype(BF16))
        o = o + _dot_nt((q * jnp.exp(cum)).astype(BF16), st.astype(BF16))
        o_ref[hh] = o
        last = cum[n - 1:n, :]
        kd = (k * jnp.exp(last - cum)).astype(BF16)
        st_s[hh] = st * jnp.exp(last) + _dot(v.T.astype(BF16), kd)
        return carry

    lax.fori_loop(0, n_heads, head, 0)

    @pl.when(c == nc - 1)
    def _():
        for hh in range(n_heads):
            s_out_ref[hh] = st_s[hh].T


def _rec_prompt(q, k, g, v, *, batch, seq):
    n_heads, m, dk = q.shape
    dv = v.shape[2]
    L = min(128, seq)
    nc = seq // L
    blk = lambda w: pl.BlockSpec((n_heads, L, w), lambda b, c: (0, b * nc + c, 0))
    kern = functools.partial(_rec_prompt_kernel, n_heads=n_heads)
    return pl.pallas_call(
        kern,
        grid=(batch, nc),
        in_specs=[blk(dk), blk(dk), blk(dk), blk(dv)],
        out_specs=[blk(dv), pl.BlockSpec((None, n_heads, dk, dv), lambda b, c: (b, 0, 0, 0))],
        out_shape=[jax.ShapeDtypeStruct((n_heads, m, dv), F32),
                   jax.ShapeDtypeStruct((batch, n_heads, dk, dv), F32)],
        scratch_shapes=[pltpu.VMEM((n_heads, dv, dk), F32)],
        compiler_params=_cparams(2),
        name="hgrn_prompt",
    )(q, k, g, v)


def _rec_sample_kernel(q_ref, k_ref, g_ref, v_ref, s0_ref, o_ref, s_out_ref, *, n_heads, t_new):
    pad = LANES - 2 * t_new
    row = lax.broadcasted_iota(jnp.int32, (t_new, LANES), 0)

    def head(hh, carry):
        q = q_ref[hh]
        k = k_ref[hh]
        v = v_ref[hh]
        cum = _cumsum_rows(g_ref[hh])
        s0 = s0_ref[hh]
        o = _dot((q * jnp.exp(cum)).astype(BF16), s0.astype(BF16))
        for s in range(t_new):
            x = jnp.where(row >= s, cum - cum[s:s + 1, :], NEG)
            a_s = jnp.sum(q * k[s:s + 1, :] * jnp.exp(x), axis=-1, keepdims=True)
            o = o + a_s * v[s:s + 1, :]
        o_ref[hh] = o
        last = cum[t_new - 1:t_new, :]
        kd = k * jnp.exp(last - cum)
        dec = jnp.broadcast_to(jnp.exp(last), (t_new, LANES))
        stack = jnp.concatenate([kd, dec, jnp.zeros((pad, LANES), F32)], axis=0).T
        vpad = jnp.concatenate([v, jnp.zeros((LANES - t_new, v.shape[1]), F32)], axis=0)
        col = lax.broadcasted_iota(jnp.int32, stack.shape, 1)
        kt = jnp.where(col < t_new, stack, 0.0).astype(BF16)
        s_out_ref[hh] = stack[:, t_new:t_new + 1] * s0 + _dot(kt, vpad.astype(BF16))
        return carry

    lax.fori_loop(0, n_heads, head, 0)


def _rec_sample(q, k, g, v, state, *, t_new):
    n_heads, m, dk = q.shape
    dv = v.shape[2]
    dec_batch = m // t_new
    blk = lambda w: pl.BlockSpec((n_heads, t_new, w), lambda b: (0, b, 0))
    st = pl.BlockSpec((None, n_heads, dk, dv), lambda b: (b, 0, 0, 0))
    kern = functools.partial(_rec_sample_kernel, n_heads=n_heads, t_new=t_new)
    return pl.pallas_call(
        kern,
        grid=(dec_batch,),
        in_specs=[blk(dk), blk(dk), blk(dk), blk(dv), st],
        out_specs=[blk(dv), st],
        out_shape=[jax.ShapeDtypeStruct((n_heads, m, dv), F32),
                   jax.ShapeDtypeStruct((dec_batch, n_heads, dk, dv), F32)],
        compiler_params=_cparams(1),
        name="hgrn_sample",
    )(q, k, g, v, state)


def _b_project(x2d, gain, w_in, w_qnorm, w_knorm, *, head_dim, tag):
    d_att = w_in.shape[1] // 4
    reps = d_att // head_dim
    gw = 2 * LANES
    gsum = (jnp.arange(gw)[:, None] // head_dim == jnp.arange(gw)[None, :] // head_dim).astype(BF16)
    qg = jnp.tile(w_qnorm, reps).reshape(1, d_att)
    kg = jnp.tile(w_knorm, reps).reshape(1, d_att)
    ep_q = functools.partial(_ep_qknorm, scale=head_dim ** -0.5, head_dim=head_dim)
    ep_k = functools.partial(_ep_qknorm, scale=1.0, head_dim=head_dim)
    (q,) = _proj(x2d, gain, w_in[:, :d_att], ep_q, (qg, gsum), 1, name="b_q_" + tag)
    (k,) = _proj(x2d, gain, w_in[:, d_att:2 * d_att], ep_k, (kg, gsum), 1, name="b_k_" + tag)
    (v,) = _proj(x2d, gain, w_in[:, 2 * d_att:3 * d_att], _ep_identity, (), 1, name="b_v_" + tag)
    (z,) = _proj(x2d, gain, w_in[:, 3 * d_att:], _ep_identity, (), 1, name="b_z_" + tag)
    return q, k, v, z


def _c_project(x2d, gain, w_in, w_lb, *, layer, d_key, d_inner, n_heads, tag):
    dk = d_key // n_heads
    ep_q = functools.partial(_ep_c_q, scale=dk ** -0.5)
    ep_f = functools.partial(_ep_c_f, layer=layer)
    (q,) = _proj(x2d, gain, w_in[:, :d_key], ep_q, (), 1, head_major=True, name="c_q_" + tag)
    k, g = _proj(x2d, gain, w_in[:, d_key:2 * d_key], ep_f, (w_lb,), 2, head_major=True,
                 name="c_f_" + tag)
    (v,) = _proj(x2d, gain, w_in[:, 2 * d_key:2 * d_key + d_inner], _ep_identity, (), 1,
                 head_major=True, name="c_v_" + tag)
    (z,) = _proj(x2d, gain, w_in[:, 2 * d_key + d_inner:], _ep_identity, (), 1, name="c_z_" + tag)
    return q, k, g, v, z


def kernel(x_prompt, x_sample, cache_k, cache_v, state_hgrn, page_table, w_norm, w_a_in, w_a_vnorm, w_a_spatial, b_a_spatial, w_a_out, w_b_in, w_b_qnorm, w_b_knorm, w_b_lambda, w_b_subln, w_b_out, w_c_in, w_c_lb, w_c_gnorm, w_c_out):
    batch, seq, d = x_prompt.shape
    dec_batch, t_new, _ = x_sample.shape
    depth = w_norm.shape[0]
    chunk_a = w_a_spatial.shape[-1]
    head_dim = w_b_qnorm.shape[-1]
    n_heads_b = w_b_out.shape[1] // (2 * head_dim)
    d_key = w_c_lb.shape[1]
    d_inner_c = w_c_out.shape[1]
    dv = w_c_gnorm.shape[-1]
    n_heads_c = d_inner_c // dv

    yp = x_prompt.reshape(batch * seq, d)
    ys = x_sample.reshape(dec_batch * t_new, d)

    a_v_s, k_p, v_p, k_s, v_s, c_p, c_s = [], [], [], [], [], [], []
    for i in range(depth):
        j = i // N_MIXERS
        kind = i % N_MIXERS
        gain = w_norm[i]
        if kind == 0:
            (yp,) = _a_layer(yp, gain, w_a_in[j], w_a_vnorm[j], w_a_spatial[j], b_a_spatial[j],
                             w_a_out[j], seg=chunk_a, write_v=False)
            ys, v_rows = _a_layer(ys, gain, w_a_in[j], w_a_vnorm[j], w_a_spatial[j],
                                  b_a_spatial[j], w_a_out[j], seg=t_new, write_v=True)
            a_v_s.append(v_rows.reshape(dec_batch, t_new, -1))
        elif kind == 1:
            lam_init = 0.8 - 0.6 * math.exp(-0.3 * i)
            q, k, v, z = _b_project(yp, gain, w_b_in[j], w_b_qnorm[j], w_b_knorm[j],
                                    head_dim=head_dim, tag="prompt")
            o = _attn_prompt(q, k, v, w_b_lambda[j], batch=batch, seq=seq, head_dim=head_dim,
                             lam_init=lam_init)
            yp = _finish(o, z, yp, w_b_subln[j], w_b_out[j], head_major=False,
                         cst=1.0 - lam_init, name="b_finish_prompt")
            k_p.append(k.reshape(batch, seq, n_heads_b, 2 * head_dim))
            v_p.append(v.reshape(batch, seq, n_heads_b, 2 * head_dim))
            q, k, v, z = _b_project(ys, gain, w_b_in[j], w_b_qnorm[j], w_b_knorm[j],
                                    head_dim=head_dim, tag="sample")
            o = _attn_sample(q, k, v, cache_k, cache_v, page_table, j, w_b_lambda[j],
                             t_new=t_new, head_dim=head_dim, lam_init=lam_init)
            ys = _finish(o, z, ys, w_b_subln[j], w_b_out[j], head_major=False,
                         cst=1.0 - lam_init, name="b_finish_sample")
            k_s.append(k.reshape(dec_batch, t_new, n_heads_b, 2 * head_dim))
            v_s.append(v.reshape(dec_batch, t_new, n_heads_b, 2 * head_dim))
        else:
            q, k, g, v, z = _c_project(yp, gain, w_c_in[j], w_c_lb, layer=i, d_key=d_key,
                                       d_inner=d_inner_c, n_heads=n_heads_c, tag="prompt")
            o, st = _rec_prompt(q, k, g, v, batch=batch, seq=seq)
            yp = _finish(o, z, yp, w_c_gnorm[j], w_c_out[j], head_major=True, cst=1.0,
                         name="c_finish_prompt")
            c_p.append(st)
            q, k, g, v, z = _c_project(ys, gain, w_c_in[j], w_c_lb, layer=i, d_key=d_key,
                                       d_inner=d_inner_c, n_heads=n_heads_c, tag="sample")
            o, st = _rec_sample(q, k, g, v, state_hgrn[j], t_new=t_new)
            ys = _finish(o, z, ys, w_c_gnorm[j], w_c_out[j], head_major=True, cst=1.0,
                         name="c_finish_sample")
            c_s.append(st)

    return (yp.reshape(batch, seq, d), ys.reshape(dec_batch, t_new, d), jnp.stack(a_v_s),
            jnp.stack(k_p), jnp.stack(v_p), jnp.stack(k_s), jnp.stack(v_s),
            jnp.stack(c_p), jnp.stack(c_s))
```
